```python
import jax, jax.numpy as jnp
from jax import lax
import numpy as np

D_MODEL = 2048
BATCH = 4
SEQ = 2048
DEPTH = 2

HEAD_DIM = 128
SB_WIDTH = D_MODEL // 2
N_SB_HEADS = SB_WIDTH // HEAD_DIM
SGU_WIDTH = D_MODEL - SB_WIDTH
N_SGU_GROUPS = 8
SGU_GROUP_DIM = SGU_WIDTH // N_SGU_GROUPS
MIX_WIDTH = SB_WIDTH + SGU_WIDTH
IN_WIDTH = 3 * SB_WIDTH + 2 * SGU_WIDTH
SB_BLOCK = 128
CHUNK = 128
D_FF = 5632
CONV_WIDTH = 3
EPS = 1e-6

kernel_name = "hybrid_stickbreaking_sgu_convffn"


def rms_norm(x, g):
    xf = x.astype(jnp.float32)
    y = xf * lax.rsqrt(jnp.mean(xf * xf, axis=-1, keepdims=True) + EPS)
    return (y * g.astype(jnp.float32)).astype(x.dtype)


def stick_breaking_attention(q, k, v):
    B, S, H, Dh = q.shape
    scale = Dh ** -0.5
    outs = []
    for i in range(S // SB_BLOCK):
        q0 = i * SB_BLOCK
        kv_len = q0 + SB_BLOCK
        q_blk = q[:, q0:kv_len]
        k_blk = k[:, :kv_len]
        v_blk = v[:, :kv_len]
        z = jnp.einsum('bthd,bshd->bhts', q_blk, k_blk,
                       preferred_element_type=jnp.float32) * scale
        t_idx = q0 + jnp.arange(SB_BLOCK)[:, None]
        s_idx = jnp.arange(kv_len)[None, :]
        mask = s_idx < t_idx
        log_beta = jax.nn.log_sigmoid(z)
        log_1m_beta = jnp.where(mask, jax.nn.log_sigmoid(-z), 0.0)
        tail = lax.cumsum(log_1m_beta, axis=3, reverse=True) - log_1m_beta
        a = jnp.where(mask, jnp.exp(log_beta + tail), 0.0)
        o = jnp.einsum('bhts,bshd->bthd', a.astype(v.dtype), v_blk)
        outs.append(o)
    return jnp.concatenate(outs, axis=1)


def chunked_spatial_gating(u, v, w_s, b_s):
    B, S, G, dg = v.shape
    n_chunks = S // CHUNK
    causal = jnp.tril(jnp.ones((CHUNK, CHUNK), dtype=bool))
    w = jnp.where(causal[None], w_s, 0.0).astype(v.dtype)
    vc = v.reshape(B, n_chunks, CHUNK, G, dg)
    mixed = jnp.einsum('gts,bcsgd->bctgd', w, vc) + b_s.T[None, None, :, :, None]
    return u * mixed.reshape(B, S, G, dg)


def causal_depthwise_conv(h, w, b):
    S = h.shape[1]
    hp = jnp.pad(h, ((0, 0), (CONV_WIDTH - 1, 0), (0, 0)))
    out = b
    for j in range(CONV_WIDTH):
        out = out + hp[:, j:j + S] * w[j]
    return out


def setup_inputs(seed: int = 0) -> dict:
    key = jax.random.key(seed)
    ks = jax.random.split(key, 20)
    f32 = jnp.float32
    nrm = lambda k, shape, s: jax.random.normal(k, shape, f32) * s
    gain = lambda k, shape: 1.0 + 0.02 * jax.random.normal(k, shape, f32)
    return {
        "x": jax.random.normal(ks[0], (BATCH, SEQ, D_MODEL), f32),
        "attn_norm_g": gain(ks[1], (DEPTH, D_MODEL)),
        "w_in": nrm(ks[2], (DEPTH, D_MODEL, IN_WIDTH), D_MODEL ** -0.5),
        "q_norm_g": gain(ks[3], (DEPTH, HEAD_DIM)),
        "k_norm_g": gain(ks[4], (DEPTH, HEAD_DIM)),
        "sgu_norm_g": gain(ks[5], (DEPTH, N_SGU_GROUPS, SGU_GROUP_DIM)),
        "sgu_w": nrm(ks[6], (DEPTH, N_SGU_GROUPS, CHUNK, CHUNK), CHUNK ** -0.5),
        "sgu_b": 1.0 + 0.02 * jax.random.normal(ks[7], (DEPTH, N_SGU_GROUPS, CHUNK), f32),
        "out_norm_a_g": gain(ks[8], (DEPTH, N_SB_HEADS, HEAD_DIM)),
        "out_norm_b_g": gain(ks[9], (DEPTH, N_SGU_GROUPS, SGU_GROUP_DIM)),
        "w_out": nrm(ks[10], (DEPTH, MIX_WIDTH, D_MODEL), (2 * DEPTH * MIX_WIDTH) ** -0.5),
        "ffn_norm_g": gain(ks[11], (DEPTH, D_MODEL)),
        "w_up": nrm(ks[12], (DEPTH, D_MODEL, 2 * D_FF), D_MODEL ** -0.5),
        "conv_w": nrm(ks[13], (DEPTH, CONV_WIDTH, 2 * D_FF), CONV_WIDTH ** -0.5),
        "conv_b": nrm(ks[14], (DEPTH, 2 * D_FF), 0.02),
        "w_down": nrm(ks[15], (DEPTH, D_FF, D_MODEL), (2 * DEPTH * D_FF) ** -0.5),
    }


def reference(x, attn_norm_g, w_in, q_norm_g, k_norm_g, sgu_norm_g, sgu_w, sgu_b,
              out_norm_a_g, out_norm_b_g, w_out, ffn_norm_g, w_up, conv_w, conv_b, w_down):
    B, S, _ = x.shape
    for l in range(DEPTH):
        h = rms_norm(x, attn_norm_g[l])
        p = h @ w_in[l]
        q, k, va, u_pre, v_pre = jnp.split(
            p, [SB_WIDTH, 2 * SB_WIDTH, 3 * SB_WIDTH, 3 * SB_WIDTH + SGU_WIDTH], axis=-1)
        q = rms_norm(q.reshape(B, S, N_SB_HEADS, HEAD_DIM), q_norm_g[l])
        k = rms_norm(k.reshape(B, S, N_SB_HEADS, HEAD_DIM), k_norm_g[l])
        va = va.reshape(B, S, N_SB_HEADS, HEAD_DIM)
        att = stick_breaking_attention(q, k, va)
        att = rms_norm(att, out_norm_a_g[l])

        u = jax.nn.gelu(u_pre, approximate=False).reshape(B, S, N_SGU_GROUPS, SGU_GROUP_DIM)
        vs = jax.nn.gelu(v_pre, approximate=False).reshape(B, S, N_SGU_GROUPS, SGU_GROUP_DIM)
        vs = rms_norm(vs, sgu_norm_g[l])
        sg = chunked_spatial_gating(u, vs, sgu_w[l], sgu_b[l])
        sg = rms_norm(sg, out_norm_b_g[l])

        mix = jnp.concatenate([att.reshape(B, S, SB_WIDTH),
                               sg.reshape(B, S, SGU_WIDTH)], axis=-1)
        x = x + mix @ w_out[l]

        h = rms_norm(x, ffn_norm_g[l])
        up = causal_depthwise_conv(h @ w_up[l], conv_w[l], conv_b[l])
        gate, val = jnp.split(up, 2, axis=-1)
        x = x + (jax.nn.silu(gate) * val) @ w_down[l]
    return x
```

```python
import functools

import jax
import jax.numpy as jnp
from jax import lax
from jax.experimental import pallas as pl
from jax.experimental.pallas import tpu as pltpu

HEAD_DIM = 128
N_HEADS = 8
N_GROUPS = 8
CHUNK = 128
CONV_WIDTH = 3
EPS = 1e-6

V7X_VMEM_BYTES = 64 * 1024 * 1024
VMEM_LIMIT_BYTES = V7X_VMEM_BYTES - 8 * 1024 * 1024
SUBLANES = 8

TM_IN = 512
TQ = 256
TK = 256
TM_OUT = 512
TM_FFN = 512
TF = 512

BF16 = jnp.bfloat16
F32 = jnp.float32


def _rms(x, g):
    ms = jnp.mean(x * x, axis=-1, keepdims=True)
    return x * lax.rsqrt(ms + EPS) * g


def _gelu(x):
    return 0.5 * x * (1.0 + lax.erf(x * 0.7071067811865476))


def _dot(a, b):
    return jnp.dot(a, b, preferred_element_type=F32)


def _resident(shape):
    return pl.BlockSpec(shape, lambda *_: (0,) * len(shape), pipeline_mode=pl.Buffered(1))


def _params(semantics):
    return pltpu.CompilerParams(dimension_semantics=semantics, vmem_limit_bytes=VMEM_LIMIT_BYTES)


def _in_proj_kernel(x_ref, g_ref, w_ref, qg_ref, kg_ref, sng_ref, sw_ref, sb_ref, ong_ref,
                    q_ref, k_ref, v_ref, sg_ref, *, sb_width):
    tm = x_ref.shape[0]
    h = _rms(x_ref[...], g_ref[...]).astype(BF16)

    def cols(i):
        return _dot(h, w_ref[:, i * sb_width:(i + 1) * sb_width])

    pq = cols(0)
    pk = cols(1)
    for hh in range(N_HEADS):
        sl = slice(hh * HEAD_DIM, (hh + 1) * HEAD_DIM)
        q_ref[:, sl] = _rms(pq[:, sl], qg_ref[...]).astype(BF16)
        k_ref[:, sl] = _rms(pk[:, sl], kg_ref[...]).astype(BF16)
    v_ref[...] = cols(2).astype(BF16)

    u = _gelu(cols(3))
    vg = _gelu(cols(4))
    row = lax.broadcasted_iota(jnp.int32, (CHUNK, CHUNK), 0)
    col = lax.broadcasted_iota(jnp.int32, (CHUNK, CHUNK), 1)
    causal = col <= row
    for g in range(N_GROUPS):
        sl = slice(g * HEAD_DIM, (g + 1) * HEAD_DIM)
        vs = _rms(vg[:, sl], sng_ref[g:g + 1, :]).astype(BF16)
        wg = jnp.where(causal, sw_ref[g], 0.0).astype(BF16)
        bias = sb_ref[g]
        for c in range(tm // CHUNK):
            rs = slice(c * CHUNK, (c + 1) * CHUNK)
            mixed = _dot(wg, vs[rs, :]) + bias
            sg = u[rs, sl] * mixed
            sg_ref[rs, sl] = _rms(sg, ong_ref[g:g + 1, :]).astype(BF16)


def _in_proj(x2, g, w_bf, qg, kg, sng, sw, sb, ong):
    t, d = x2.shape
    sbw = N_HEADS * HEAD_DIM
    row_blk = lambda w: pl.BlockSpec((TM_IN, w), lambda m: (m, 0))
    out = jax.ShapeDtypeStruct((t, sbw), BF16)
    return pl.pallas_call(
        functools.partial(_in_proj_kernel, sb_width=sbw),
        grid=(t // TM_IN,),
        in_specs=[
            row_blk(d),
            _resident((1, d)),
            _resident(w_bf.shape),
            _resident((1, HEAD_DIM)),
            _resident((1, HEAD_DIM)),
            _resident((N_GROUPS, HEAD_DIM)),
            _resident((N_GROUPS, CHUNK, CHUNK)),
            _resident((N_GROUPS, CHUNK, 1)),
            _resident((N_GROUPS, HEAD_DIM)),
        ],
        out_specs=[row_blk(sbw)] * 4,
        out_shape=[out] * 4,
        compiler_params=_params(("arbitrary",)),
        name="in_proj",
    )(x2, g, w_bf, qg, kg, sng, sw, sb, ong)


def _attn_kernel(q_ref, k_ref, v_ref, g_ref, o_ref, acc_ref, tail_ref):
    qi = pl.program_id(2)
    q = q_ref[...]
    scale = HEAD_DIM ** -0.5

    row = lax.broadcasted_iota(jnp.int32, (TK, TK), 0)
    col = lax.broadcasted_iota(jnp.int32, (TK, TK), 1)
    later = jnp.where(row > col, 1.0, 0.0).astype(BF16)
    rhs = jnp.concatenate([later, jnp.ones((TK, HEAD_DIM), BF16)], axis=1)
    causal = col < row

    acc_ref[...] = jnp.zeros_like(acc_ref)
    tail_ref[...] = jnp.zeros_like(tail_ref)

    def block(j, masked):
        start = pl.multiple_of(j * TK, TK)
        kj = k_ref[pl.ds(start, TK), :]
        vj = v_ref[pl.ds(start, TK), :]
        z = lax.dot_general(q, kj, (((1,), (1,)), ((), ())), preferred_element_type=F32) * scale
        sp = jnp.maximum(z, 0.0) + jnp.log1p(jnp.exp(-jnp.abs(z)))
        log_beta = z - sp
        log_1m = -sp
        if masked:
            log_1m = jnp.where(causal, log_1m, 0.0)
        hi = log_1m.astype(BF16)
        lo = (log_1m - hi.astype(F32)).astype(BF16)
        sums = _dot(hi, rhs) + _dot(lo, rhs)
        carried = tail_ref[...]
        tail = sums[:, :TK] + jnp.concatenate([carried] * (TK // HEAD_DIM), axis=1)
        a = jnp.exp(log_beta + tail)
        if masked:
            a = jnp.where(causal, a, 0.0)
        acc_ref[...] += _dot(a.astype(BF16), vj)
        tail_ref[...] = carried + sums[:, TK:]

    block(qi, True)

    def body(i, carry):
        block(qi - 1 - i, False)
        return carry

    lax.fori_loop(0, qi, body, 0)
    o_ref[...] = _rms(acc_ref[...], g_ref[0]).astype(BF16)


def _attention(q, k, v, ong, batch, seq):
    t, w = q.shape
    nq = seq // TQ
    return pl.pallas_call(
        _attn_kernel,
        grid=(batch, N_HEADS, nq),
        in_specs=[
            pl.BlockSpec((TQ, HEAD_DIM), lambda b, h, i: (b * nq + i, h)),
            pl.BlockSpec((seq, HEAD_DIM), lambda b, h, i: (b, h)),
            pl.BlockSpec((seq, HEAD_DIM), lambda b, h, i: (b, h)),
            pl.BlockSpec((1, 1, HEAD_DIM), lambda b, h, i: (h, 0, 0)),
        ],
        out_specs=pl.BlockSpec((TQ, HEAD_DIM), lambda b, h, i: (b * nq + i, h)),
        out_shape=jax.ShapeDtypeStruct((t, w), BF16),
        scratch_shapes=[pltpu.VMEM((TQ, HEAD_DIM), F32), pltpu.VMEM((TQ, HEAD_DIM), F32)],
        compiler_params=_params(("arbitrary", "arbitrary", "arbitrary")),
        name="sb_attn",
    )(q, k, v, ong)


def _out_proj_kernel(att_ref, sg_ref, x_ref, w_ref, o_ref):
    half = att_ref.shape[1]
    o_ref[...] = x_ref[...] + (_dot(att_ref[...], w_ref[:half, :]) + _dot(sg_ref[...], w_ref[half:, :]))


def _out_proj(att, sg, x2, w_bf):
    t, d = x2.shape
    half = att.shape[1]
    return pl.pallas_call(
        _out_proj_kernel,
        grid=(t // TM_OUT,),
        in_specs=[
            pl.BlockSpec((TM_OUT, half), lambda m: (m, 0)),
            pl.BlockSpec((TM_OUT, half), lambda m: (m, 0)),
            pl.BlockSpec((TM_OUT, d), lambda m: (m, 0)),
            _resident(w_bf.shape),
        ],
        out_specs=pl.BlockSpec((TM_OUT, d), lambda m: (m, 0)),
        out_shape=jax.ShapeDtypeStruct((t, d), F32),
        compiler_params=_params(("arbitrary",)),
        name="out_proj",
    )(att, sg, x2, w_bf)


def _causal_conv(u, prev, cw, cb):
    tm = u.shape[0]
    ext = jnp.concatenate([prev, u], axis=0)
    u1 = ext[SUBLANES - 1:SUBLANES - 1 + tm]
    u2 = ext[SUBLANES - 2:SUBLANES - 2 + tm]
    return cb + u2 * cw[0:1, :] + u1 * cw[1:2, :] + u * cw[2:3, :]


def _ffn_kernel(x_ref, g_ref, wg_ref, wv_ref, cwg_ref, cwv_ref, cbg_ref, cbv_ref, wd_ref,
                o_ref, h_ref, pg_ref, pv_ref, *, tiles_per_seq):
    m = pl.program_id(0)
    f = pl.program_id(1)
    tm = x_ref.shape[0]

    @pl.when(f == 0)
    def _():
        x = x_ref[...]
        h_ref[...] = _rms(x, g_ref[...]).astype(BF16)
        o_ref[...] = x

    h = h_ref[...]
    ug = _dot(h, wg_ref[...])
    uv = _dot(h, wv_ref[...])
    @pl.when(m % tiles_per_seq == 0)
    def _():
        pg_ref[f] = jnp.zeros((SUBLANES, pg_ref.shape[2]), F32)
        pv_ref[f] = jnp.zeros((SUBLANES, pv_ref.shape[2]), F32)

    gate = _causal_conv(ug, pg_ref[f], cwg_ref[...], cbg_ref[...])
    val = _causal_conv(uv, pv_ref[f], cwv_ref[...], cbv_ref[...])
    pg_ref[f] = ug[tm - SUBLANES:, :]
    pv_ref[f] = uv[tm - SUBLANES:, :]
    act = (gate * jax.nn.sigmoid(gate) * val).astype(BF16)
    o_ref[...] += _dot(act, wd_ref[...])


def _ffn(x2, g, wu_bf, cw, cb, wd_bf, seq):
    t, d = x2.shape
    dff = wd_bf.shape[0]
    nf = dff // TF
    return pl.pallas_call(
        functools.partial(_ffn_kernel, tiles_per_seq=seq // TM_FFN),
        grid=(t // TM_FFN, nf),
        in_specs=[
            pl.BlockSpec((TM_FFN, d), lambda m, f: (m, 0)),
            pl.BlockSpec((1, d), lambda m, f: (0, 0)),
            pl.BlockSpec((d, TF), lambda m, f: (0, f)),
            pl.BlockSpec((d, TF), lambda m, f: (0, nf + f)),
            pl.BlockSpec((CONV_WIDTH, TF), lambda m, f: (0, f)),
            pl.BlockSpec((CONV_WIDTH, TF), lambda m, f: (0, nf + f)),
            pl.BlockSpec((1, TF), lambda m, f: (0, f)),
            pl.BlockSpec((1, TF), lambda m, f: (0, nf + f)),
            pl.BlockSpec((TF, d), lambda m, f: (f, 0)),
        ],
        out_specs=pl.BlockSpec((TM_FFN, d), lambda m, f: (m, 0)),
        out_shape=jax.ShapeDtypeStruct((t, d), F32),
        scratch_shapes=[
            pltpu.VMEM((TM_FFN, d), BF16),
            pltpu.VMEM((nf, SUBLANES, TF), F32),
            pltpu.VMEM((nf, SUBLANES, TF), F32),
        ],
        compiler_params=_params(("arbitrary", "arbitrary")),
        name="conv_ffn",
    )(x2, g, wu_bf, wu_bf, cw, cw, cb, cb, wd_bf)


def kernel(x, attn_norm_g, w_in, q_norm_g, k_norm_g, sgu_norm_g, sgu_w, sgu_b, out_norm_a_g,
           out_norm_b_g, w_out, ffn_norm_g, w_up, conv_w, conv_b, w_down):
    batch, seq, d = x.shape
    depth = w_in.shape[0]
    assert seq % TM_FFN == 0 and seq % TQ == 0 and seq % TM_IN == 0 and TM_IN % CHUNK == 0
    assert TQ == TK and w_down.shape[1] % TF == 0
    x2 = x.reshape(batch * seq, d)
    for l in range(depth):
        q, k, v, sg = _in_proj(
            x2, attn_norm_g[l][None], w_in[l].astype(BF16), q_norm_g[l][None], k_norm_g[l][None],
            sgu_norm_g[l], sgu_w[l], sgu_b[l][:, :, None], out_norm_b_g[l])
        att = _attention(q, k, v, out_norm_a_g[l][:, None, :], batch, seq)
        x2 = _out_proj(att, sg, x2, w_out[l].astype(BF16))
        x2 = _ffn(x2, ffn_norm_g[l][None], w_up[l].astype(BF16), conv_w[l], conv_b[l][None],
                  w_down[l].astype(BF16), seq)
    return x2.reshape(batch, seq, d)
```

```python
import functools

import jax
import jax.numpy as jnp
from jax import lax
from jax.experimental import pallas as pl
from jax.experimental.pallas import tpu as pltpu

HEAD_DIM = 128
N_HEADS = 8
N_GROUPS = 8
CHUNK = 128
CONV_WIDTH = 3
EPS = 1e-6

V7X_VMEM_BYTES = 64 * 1024 * 1024
VMEM_LIMIT_BYTES = V7X_VMEM_BYTES - 8 * 1024 * 1024
SUBLANES = 8

TM_IN = 512
TQ = 256
TK = 256
HEADS_PER_STEP = 4
TM_OUT = 512
TM_FFN = 512
TF = 512
FFN_CHUNK = 256

BF16 = jnp.bfloat16
F32 = jnp.float32


def _rms(x, g):
    ms = jnp.mean(x * x, axis=-1, keepdims=True)
    return x * lax.rsqrt(ms + EPS) * g


def _gelu(x):
    return 0.5 * x * (1.0 + lax.erf(x * 0.7071067811865476))


def _dot(a, b):
    return jnp.dot(a, b, preferred_element_type=F32)


def _resident(shape):
    return pl.BlockSpec(shape, lambda *_: (0,) * len(shape), pipeline_mode=pl.Buffered(1))


def _params(semantics):
    return pltpu.CompilerParams(dimension_semantics=semantics, vmem_limit_bytes=VMEM_LIMIT_BYTES)


def _in_proj_kernel(x_ref, g_ref, w_ref, qg_ref, kg_ref, sng_ref, sw_ref, sb_ref, ong_ref,
                    q_ref, k_ref, v_ref, sg_ref, *, sb_width):
    tm = x_ref.shape[0]
    h = _rms(x_ref[...], g_ref[...]).astype(BF16)

    def cols(i):
        return _dot(h, w_ref[:, i * sb_width:(i + 1) * sb_width])

    pq = cols(0)
    pk = cols(1)
    for hh in range(N_HEADS):
        sl = slice(hh * HEAD_DIM, (hh + 1) * HEAD_DIM)
        q_ref[:, sl] = _rms(pq[:, sl], qg_ref[...]).astype(BF16)
        k_ref[:, sl] = _rms(pk[:, sl], kg_ref[...]).astype(BF16)
    v_ref[...] = cols(2).astype(BF16)

    u = _gelu(cols(3))
    vg = _gelu(cols(4))
    row = lax.broadcasted_iota(jnp.int32, (CHUNK, CHUNK), 0)
    col = lax.broadcasted_iota(jnp.int32, (CHUNK, CHUNK), 1)
    causal = col <= row
    for g in range(N_GROUPS):
        sl = slice(g * HEAD_DIM, (g + 1) * HEAD_DIM)
        vs = _rms(vg[:, sl], sng_ref[g:g + 1, :]).astype(BF16)
        wg = jnp.where(causal, sw_ref[g], 0.0).astype(BF16)
        bias = sb_ref[g]
        for c in range(tm // CHUNK):
            rs = slice(c * CHUNK, (c + 1) * CHUNK)
            mixed = _dot(wg, vs[rs, :]) + bias
            sg = u[rs, sl] * mixed
            sg_ref[rs, sl] = _rms(sg, ong_ref[g:g + 1, :]).astype(BF16)


def _in_proj(x2, g, w_bf, qg, kg, sng, sw, sb, ong):
    t, d = x2.shape
    sbw = N_HEADS * HEAD_DIM
    row_blk = lambda w: pl.BlockSpec((TM_IN, w), lambda m: (m, 0))
    out = jax.ShapeDtypeStruct((t, sbw), BF16)
    return pl.pallas_call(
        functools.partial(_in_proj_kernel, sb_width=sbw),
        grid=(t // TM_IN,),
        in_specs=[
            row_blk(d),
            _resident((1, d)),
            _resident(w_bf.shape),
            _resident((1, HEAD_DIM)),
            _resident((1, HEAD_DIM)),
            _resident((N_GROUPS, HEAD_DIM)),
            _resident((N_GROUPS, CHUNK, CHUNK)),
            _resident((N_GROUPS, CHUNK, 1)),
            _resident((N_GROUPS, HEAD_DIM)),
        ],
        out_specs=[row_blk(sbw)] * 4,
        out_shape=[out] * 4,
        compiler_params=_params(("arbitrary",)),
        name="in_proj",
    )(x2, g, w_bf, qg, kg, sng, sw, sb, ong)


def _attn_kernel(q_ref, k_ref, v_ref, g_ref, o_ref, acc_ref, tail_ref):
    qi = pl.program_id(2)
    scale = HEAD_DIM ** -0.5
    row = lax.broadcasted_iota(jnp.int32, (TK, TK), 0)
    col = lax.broadcasted_iota(jnp.int32, (TK, TK), 1)
    later = jnp.where(row > col, 1.0, 0.0).astype(BF16)
    later2 = jnp.concatenate([later, later], axis=0)
    causal = col < row

    acc_ref[...] = jnp.zeros_like(acc_ref)
    tail_ref[...] = jnp.zeros_like(tail_ref)

    heads = range(HEADS_PER_STEP)
    lanes = [slice(hh * HEAD_DIM, (hh + 1) * HEAD_DIM) for hh in heads]

    def block(j, masked):
        start = pl.multiple_of(j * TK, TK)
        z = [lax.dot_general(q_ref[:, cs], k_ref[pl.ds(start, TK), cs], (((1,), (1,)), ((), ())),
                             preferred_element_type=F32) * scale for cs in lanes]
        log_beta, log_1m, split = [], [], []
        for hh in heads:
            neg_abs = lax.bitcast_convert_type(
                lax.bitcast_convert_type(z[hh], jnp.uint32) | jnp.uint32(0x80000000), F32)
            lb = jnp.minimum(z[hh], 0.0) - jnp.log(1.0 + jnp.exp(neg_abs))
            l1 = lb - z[hh]
            if masked:
                l1 = jnp.where(causal, l1, 0.0)
            hi = lax.bitcast_convert_type(
                lax.bitcast_convert_type(l1, jnp.uint32) & jnp.uint32(0xFFFF0000), F32)
            split.append(jnp.concatenate([hi.astype(BF16), (l1 - hi).astype(BF16)], axis=1))
            log_beta.append(lb)
            log_1m.append(l1)
        inner = [_dot(split[hh], later2) for hh in heads]
        a = []
        for hh in heads:
            carried = tail_ref[hh]
            tail = inner[hh] + jnp.concatenate([carried] * (TK // HEAD_DIM), axis=1)
            w = jnp.exp(log_beta[hh] + tail)
            if masked:
                w = jnp.where(causal, w, 0.0)
            a.append(w.astype(BF16))
            tail_ref[hh] = carried + jnp.sum(log_1m[hh], axis=1, keepdims=True)
        for hh in heads:
            acc_ref[hh] += _dot(a[hh], v_ref[pl.ds(start, TK), lanes[hh]])

    block(qi, True)

    def body(i, carry):
        block(qi - 1 - i, False)
        return carry

    lax.fori_loop(0, qi, body, 0)
    for hh in range(HEADS_PER_STEP):
        cs = slice(hh * HEAD_DIM, (hh + 1) * HEAD_DIM)
        o_ref[:, cs] = _rms(acc_ref[hh], g_ref[0, hh:hh + 1, :]).astype(BF16)


def _attention(q, k, v, ong, batch, seq):
    t, w = q.shape
    nq = seq // TQ
    hw = HEADS_PER_STEP * HEAD_DIM
    return pl.pallas_call(
        _attn_kernel,
        grid=(batch, N_HEADS // HEADS_PER_STEP, nq),
        in_specs=[
            pl.BlockSpec((TQ, hw), lambda b, h, i: (b * nq + i, h)),
            pl.BlockSpec((seq, hw), lambda b, h, i: (b, h)),
            pl.BlockSpec((seq, hw), lambda b, h, i: (b, h)),
            pl.BlockSpec((1, HEADS_PER_STEP, HEAD_DIM), lambda b, h, i: (h, 0, 0)),
        ],
        out_specs=pl.BlockSpec((TQ, hw), lambda b, h, i: (b * nq + i, h)),
        out_shape=jax.ShapeDtypeStruct((t, w), BF16),
        scratch_shapes=[pltpu.VMEM((HEADS_PER_STEP, TQ, HEAD_DIM), F32),
                        pltpu.VMEM((HEADS_PER_STEP, TQ, HEAD_DIM), F32)],
        compiler_params=_params(("arbitrary", "arbitrary", "arbitrary")),
        name="sb_attn",
    )(q, k, v, ong)


def _out_proj_kernel(att_ref, sg_ref, x_ref, w_ref, o_ref):
    half = att_ref.shape[1]
    o_ref[...] = x_ref[...] + (_dot(att_ref[...], w_ref[:half, :]) + _dot(sg_ref[...], w_ref[half:, :]))


def _out_proj(att, sg, x2, w_bf):
    t, d = x2.shape
    half = att.shape[1]
    return pl.pallas_call(
        _out_proj_kernel,
        grid=(t // TM_OUT,),
        in_specs=[
            pl.BlockSpec((TM_OUT, half), lambda m: (m, 0)),
            pl.BlockSpec((TM_OUT, half), lambda m: (m, 0)),
            pl.BlockSpec((TM_OUT, d), lambda m: (m, 0)),
            _resident(w_bf.shape),
        ],
        out_specs=pl.BlockSpec((TM_OUT, d), lambda m: (m, 0)),
        out_shape=jax.ShapeDtypeStruct((t, d), F32),
        compiler_params=_params(("arbitrary",)),
        name="out_proj",
    )(att, sg, x2, w_bf)


def _causal_conv(u, prev_ref, cw, cb):
    tm = u.shape[0]
    ext = jnp.concatenate([prev_ref[...], u], axis=0)
    prev_ref[...] = u[tm - SUBLANES:, :]
    u1 = pltpu.roll(ext, 1, axis=0)[SUBLANES:, :]
    u2 = pltpu.roll(ext, 2, axis=0)[SUBLANES:, :]
    return cb + u2 * cw[0:1, :] + u1 * cw[1:2, :] + u * cw[2:3, :]


def _ffn_kernel(x_ref, g_ref, wg_ref, wv_ref, cwg_ref, cwv_ref, cbg_ref, cbv_ref, wd_ref,
                o_ref, h_ref, pg_ref, pv_ref, *, tiles_per_seq):
    m = pl.program_id(0)
    f = pl.program_id(1)

    @pl.when(f == 0)
    def _():
        x = x_ref[...]
        h_ref[...] = _rms(x, g_ref[...]).astype(BF16)
        o_ref[...] = x

    @pl.when(m % tiles_per_seq == 0)
    def _():
        pg_ref[f] = jnp.zeros(pg_ref.shape[1:], F32)
        pv_ref[f] = jnp.zeros(pv_ref.shape[1:], F32)

    chunks = [slice(c * FFN_CHUNK, (c + 1) * FFN_CHUNK) for c in range(TF // FFN_CHUNK)]
    up = [(_dot(h_ref[...], wg_ref[:, cl]), _dot(h_ref[...], wv_ref[:, cl])) for cl in chunks]
    down = None
    for cl, (ug, uv) in zip(chunks, up):
        gate = _causal_conv(ug, pg_ref.at[f, :, cl], cwg_ref[:, cl], cbg_ref[:, cl])
        val = _causal_conv(uv, pv_ref.at[f, :, cl], cwv_ref[:, cl], cbv_ref[:, cl])
        act = (gate * jax.nn.sigmoid(gate) * val).astype(BF16)
        part = _dot(act, wd_ref[cl, :])
        down = part if down is None else down + part
    o_ref[...] += down


def _ffn(x2, g, wu_bf, cw, cb, wd_bf, seq):
    t, d = x2.shape
    dff = wd_bf.shape[0]
    nf = dff // TF
    return pl.pallas_call(
        functools.partial(_ffn_kernel, tiles_per_seq=seq // TM_FFN),
        grid=(t // TM_FFN, nf),
        in_specs=[
            pl.BlockSpec((TM_FFN, d), lambda m, f: (m, 0)),
            pl.BlockSpec((1, d), lambda m, f: (0, 0)),
            pl.BlockSpec((d, TF), lambda m, f: (0, f)),
            pl.BlockSpec((d, TF), lambda m, f: (0, nf + f)),
            pl.BlockSpec((CONV_WIDTH, TF), lambda m, f: (0, f)),
            pl.BlockSpec((CONV_WIDTH, TF), lambda m, f: (0, nf + f)),
            pl.BlockSpec((1, TF), lambda m, f: (0, f)),
            pl.BlockSpec((1, TF), lambda m, f: (0, nf + f)),
            pl.BlockSpec((TF, d), lambda m, f: (f, 0)),
        ],
        out_specs=pl.BlockSpec((TM_FFN, d), lambda m, f: (m, 0)),
        out_shape=jax.ShapeDtypeStruct((t, d), F32),
        scratch_shapes=[
            pltpu.VMEM((TM_FFN, d), BF16),
            pltpu.VMEM((nf, SUBLANES, TF), F32),
            pltpu.VMEM((nf, SUBLANES, TF), F32),
        ],
        compiler_params=_params(("arbitrary", "arbitrary")),
        name="conv_ffn",
    )(x2, g, wu_bf, wu_bf, cw, cw, cb, cb, wd_bf)


def kernel(x, attn_norm_g, w_in, q_norm_g, k_norm_g, sgu_norm_g, sgu_w, sgu_b, out_norm_a_g,
           out_norm_b_g, w_out, ffn_norm_g, w_up, conv_w, conv_b, w_down):
    batch, seq, d = x.shape
    depth = w_in.shape[0]
    assert seq % TM_FFN == 0 and seq % TQ == 0 and seq % TM_IN == 0 and TM_IN % CHUNK == 0
    assert TQ == TK and w_down.shape[1] % TF == 0
    x2 = x.reshape(batch * seq, d)
    for l in range(depth):
        q, k, v, sg = _in_proj(
            x2, attn_norm_g[l][None], w_in[l].astype(BF16), q_norm_g[l][None], k_norm_g[l][None],
            sgu_norm_g[l], sgu_w[l], sgu_b[l][:, :, None], out_norm_b_g[l])
        att = _attention(
            q, k, v, out_norm_a_g[l].reshape(N_HEADS // HEADS_PER_STEP, HEADS_PER_STEP, HEAD_DIM),
            batch, seq)
        x2 = _out_proj(att, sg, x2, w_out[l].astype(BF16))
        x2 = _ffn(x2, ffn_norm_g[l][None], w_up[l].astype(BF16), conv_w[l], conv_b[l][None],
                  w_down[l].astype(BF16), seq)
    return x2.reshape(batch, seq, d)
```

```python
import functools

import jax
import jax.numpy as jnp
from jax import lax
from jax.experimental import pallas as pl
from jax.experimental.pallas import tpu as pltpu

HEAD_DIM = 128
N_HEADS = 8
N_GROUPS = 8
CHUNK = 128
CONV_WIDTH = 3
EPS = 1e-6
F32_EXP_UNDERFLOW = -105.0

V7X_VMEM_BYTES = 64 * 1024 * 1024
VMEM_LIMIT_BYTES = V7X_VMEM_BYTES - 8 * 1024 * 1024
SUBLANES = 8

TM_IN = 512
TQ = 256
TK = 256
HEADS_PER_STEP = 4
TM_OUT = 512
TM_FFN = 1024
FFN_CHUNK = 256
FFN_ROWS = 256

BF16 = jnp.bfloat16
F32 = jnp.float32


def _rms(x, g):
    ms = jnp.mean(x * x, axis=-1, keepdims=True)
    return x * lax.rsqrt(ms + EPS) * g


def _gelu(x):
    return 0.5 * x * (1.0 + lax.erf(x * 0.7071067811865476))


def _dot(a, b):
    return jnp.dot(a, b, preferred_element_type=F32)


def _resident(shape):
    return pl.BlockSpec(shape, lambda *_: (0,) * len(shape), pipeline_mode=pl.Buffered(1))


def _params(semantics):
    return pltpu.CompilerParams(dimension_semantics=semantics, vmem_limit_bytes=VMEM_LIMIT_BYTES)


def _in_proj_kernel(x_ref, g_ref, w_ref, qg_ref, kg_ref, sng_ref, sw_ref, sb_ref, ong_ref,
                    q_ref, k_ref, v_ref, sg_ref, *, sb_width):
    tm = x_ref.shape[0]
    h = _rms(x_ref[...], g_ref[...]).astype(BF16)

    def cols(i):
        return _dot(h, w_ref[:, i * sb_width:(i + 1) * sb_width])

    pq = cols(0)
    pk = cols(1)
    for hh in range(N_HEADS):
        sl = slice(hh * HEAD_DIM, (hh + 1) * HEAD_DIM)
        q_ref[:, sl] = _rms(pq[:, sl], qg_ref[...]).astype(BF16)
        k_ref[:, sl] = _rms(pk[:, sl], kg_ref[...]).astype(BF16)
    v_ref[...] = cols(2).astype(BF16)

    u = _gelu(cols(3))
    vg = _gelu(cols(4))
    row = lax.broadcasted_iota(jnp.int32, (CHUNK, CHUNK), 0)
    col = lax.broadcasted_iota(jnp.int32, (CHUNK, CHUNK), 1)
    causal = col <= row
    for g in range(N_GROUPS):
        sl = slice(g * HEAD_DIM, (g + 1) * HEAD_DIM)
        vs = _rms(vg[:, sl], sng_ref[g:g + 1, :]).astype(BF16)
        wg = jnp.where(causal, sw_ref[g], 0.0).astype(BF16)
        bias = sb_ref[g]
        for c in range(tm // CHUNK):
            rs = slice(c * CHUNK, (c + 1) * CHUNK)
            mixed = _dot(wg, vs[rs, :]) + bias
            sg = u[rs, sl] * mixed
            sg_ref[rs, sl] = _rms(sg, ong_ref[g:g + 1, :]).astype(BF16)


def _in_proj(x2, g, w_bf, qg, kg, sng, sw, sb, ong):
    t, d = x2.shape
    sbw = N_HEADS * HEAD_DIM
    row_blk = lambda w: pl.BlockSpec((TM_IN, w), lambda m: (m, 0))
    out = jax.ShapeDtypeStruct((t, sbw), BF16)
    return pl.pallas_call(
        functools.partial(_in_proj_kernel, sb_width=sbw),
        grid=(t // TM_IN,),
        in_specs=[
            row_blk(d),
            _resident((1, d)),
            _resident(w_bf.shape),
            _resident((1, HEAD_DIM)),
            _resident((1, HEAD_DIM)),
            _resident((N_GROUPS, HEAD_DIM)),
            _resident((N_GROUPS, CHUNK, CHUNK)),
            _resident((N_GROUPS, CHUNK, 1)),
            _resident((N_GROUPS, HEAD_DIM)),
        ],
        out_specs=[row_blk(sbw)] * 4,
        out_shape=[out] * 4,
        compiler_params=_params(("arbitrary",)),
        name="in_proj",
    )(x2, g, w_bf, qg, kg, sng, sw, sb, ong)


def _attn_kernel(q_ref, k_ref, v_ref, g_ref, o_ref, acc_ref, tail_ref):
    qi = pl.program_id(2)
    scale = HEAD_DIM ** -0.5
    row = lax.broadcasted_iota(jnp.int32, (TK, TK), 0)
    col = lax.broadcasted_iota(jnp.int32, (TK, TK), 1)
    later = jnp.where(row > col, 1.0, 0.0).astype(BF16)
    later2 = jnp.concatenate([later, later], axis=0)
    causal = col < row

    acc_ref[...] = jnp.zeros_like(acc_ref)
    tail_ref[...] = jnp.zeros_like(tail_ref)

    heads = range(HEADS_PER_STEP)
    lanes = [slice(hh * HEAD_DIM, (hh + 1) * HEAD_DIM) for hh in heads]

    def block(j, masked):
        start = pl.multiple_of(j * TK, TK)
        z = [lax.dot_general(q_ref[:, cs], k_ref[pl.ds(start, TK), cs], (((1,), (1,)), ((), ())),
                             preferred_element_type=F32) * scale for cs in lanes]
        log_beta, log_1m, split = [], [], []
        for hh in heads:
            lb = jnp.minimum(z[hh], 0.0) - jnp.log(1.0 + jnp.exp(-jnp.abs(z[hh])))
            l1 = lb - z[hh]
            if masked:
                l1 = jnp.where(causal, l1, 0.0)
            hi = l1.astype(BF16)
            split.append(jnp.concatenate([hi, (l1 - hi.astype(F32)).astype(BF16)], axis=1))
            log_beta.append(lb)
            log_1m.append(l1)
        inner = [_dot(split[hh], later2) for hh in heads]
        a = []
        live = None
        for hh in heads:
            carried = tail_ref[hh]
            tail = inner[hh] + jnp.concatenate([carried] * (TK // HEAD_DIM), axis=1)
            w = jnp.exp(log_beta[hh] + tail)
            if masked:
                w = jnp.where(causal, w, 0.0)
            a.append(w.astype(BF16))
            swept = carried + jnp.sum(log_1m[hh], axis=1, keepdims=True)
            tail_ref[hh] = swept
            live = swept if live is None else jnp.maximum(live, swept)
        for hh in heads:
            acc_ref[hh] += _dot(a[hh], v_ref[pl.ds(start, TK), lanes[hh]])
        return jnp.max(live) > F32_EXP_UNDERFLOW

    def cond(state):
        i, live = state
        return jnp.logical_and(i < qi, live)

    def body(state):
        i, _ = state
        return i + 1, block(qi - 1 - i, False)

    lax.while_loop(cond, body, (jnp.int32(0), block(qi, True)))
    for hh in range(HEADS_PER_STEP):
        cs = slice(hh * HEAD_DIM, (hh + 1) * HEAD_DIM)
        o_ref[:, cs] = _rms(acc_ref[hh], g_ref[0, hh:hh + 1, :]).astype(BF16)


def _attention(q, k, v, ong, batch, seq):
    t, w = q.shape
    nq = seq // TQ
    hw = HEADS_PER_STEP * HEAD_DIM
    return pl.pallas_call(
        _attn_kernel,
        grid=(batch, N_HEADS // HEADS_PER_STEP, nq),
        in_specs=[
            pl.BlockSpec((TQ, hw), lambda b, h, i: (b * nq + i, h)),
            pl.BlockSpec((seq, hw), lambda b, h, i: (b, h)),
            pl.BlockSpec((seq, hw), lambda b, h, i: (b, h)),
            pl.BlockSpec((1, HEADS_PER_STEP, HEAD_DIM), lambda b, h, i: (h, 0, 0)),
        ],
        out_specs=pl.BlockSpec((TQ, hw), lambda b, h, i: (b * nq + i, h)),
        out_shape=jax.ShapeDtypeStruct((t, w), BF16),
        scratch_shapes=[pltpu.VMEM((HEADS_PER_STEP, TQ, HEAD_DIM), F32),
                        pltpu.VMEM((HEADS_PER_STEP, TQ, HEAD_DIM), F32)],
        compiler_params=_params(("arbitrary", "arbitrary", "arbitrary")),
        name="sb_attn",
    )(q, k, v, ong)


def _out_proj_kernel(att_ref, sg_ref, x_ref, w_ref, o_ref):
    half = att_ref.shape[1]
    o_ref[...] = x_ref[...] + (_dot(att_ref[...], w_ref[:half, :]) + _dot(sg_ref[...], w_ref[half:, :]))


def _out_proj(att, sg, x2, w_bf):
    t, d = x2.shape
    half = att.shape[1]
    return pl.pallas_call(
        _out_proj_kernel,
        grid=(t // TM_OUT,),
        in_specs=[
            pl.BlockSpec((TM_OUT, half), lambda m: (m, 0)),
            pl.BlockSpec((TM_OUT, half), lambda m: (m, 0)),
            pl.BlockSpec((TM_OUT, d), lambda m: (m, 0)),
            _resident(w_bf.shape),
        ],
        out_specs=pl.BlockSpec((TM_OUT, d), lambda m: (m, 0)),
        out_shape=jax.ShapeDtypeStruct((t, d), F32),
        compiler_params=_params(("arbitrary",)),
        name="out_proj",
    )(att, sg, x2, w_bf)


def _causal_conv(ext, cw, cb):
    u = ext[SUBLANES:, :]
    u1 = pltpu.roll(ext, 1, axis=0)[SUBLANES:, :]
    u2 = pltpu.roll(ext, 2, axis=0)[SUBLANES:, :]
    return cb + u2 * cw[0:1, :] + u1 * cw[1:2, :] + u * cw[2:3, :]


def _ffn_kernel(x_hbm, g_ref, wup_hbm, cw_ref, cb_ref, wdn_hbm, o_ref,
                h_ref, wg_st, wv_st, wd_st, wg_bf, wv_bf, wd_bf, ug_ref, uv_ref, pg_ref, pv_ref,
                sem_x, sem_w,
                *, layer, tiles_per_seq):
    m = pl.program_id(0)
    n_steps = pl.num_programs(0)
    tm = o_ref.shape[0]
    d_ff = wdn_hbm.shape[1]
    nf = d_ff // FFN_CHUNK

    def weight_copies(f, slot):
        col = pl.multiple_of(f * FFN_CHUNK, FFN_CHUNK)
        return (
            pltpu.make_async_copy(wup_hbm.at[layer, :, pl.ds(col, FFN_CHUNK)],
                                  wg_st.at[slot], sem_w.at[slot, 0]),
            pltpu.make_async_copy(wup_hbm.at[layer, :, pl.ds(pl.multiple_of(d_ff + col, FFN_CHUNK), FFN_CHUNK)],
                                  wv_st.at[slot], sem_w.at[slot, 1]),
            pltpu.make_async_copy(wdn_hbm.at[layer, pl.ds(col, FFN_CHUNK), :],
                                  wd_st.at[slot], sem_w.at[slot, 2]),
        )

    def fetch(f, slot):
        for c in weight_copies(f, slot):
            c.wait()
        nxt = f + 1

        @pl.when(jnp.logical_or(nxt < nf, m < n_steps - 1))
        def _():
            for c in weight_copies(jnp.where(nxt < nf, nxt, 0), 1 - slot):
                c.start()

    def rows_before(u_ref, slot, prev_ref, f, r):
        if r == 0:
            return prev_ref[f]
        return u_ref[slot, r * FFN_ROWS - SUBLANES:r * FFN_ROWS, :]

    def stage(up_f, up_slot, down_f, down_slot):
        if up_slot is not None:
            fetch(up_f, up_slot)
            wg_bf[...] = wg_st[up_slot].astype(BF16)
            wv_bf[...] = wv_st[up_slot].astype(BF16)
            wd_bf[up_slot] = wd_st[up_slot].astype(BF16)
        if down_f is not None:
            col = pl.multiple_of(down_f * FFN_CHUNK, FFN_CHUNK)
            vcol = pl.multiple_of(d_ff + col, FFN_CHUNK)
            cwg, cbg = cw_ref[:, pl.ds(col, FFN_CHUNK)], cb_ref[:, pl.ds(col, FFN_CHUNK)]
            cwv, cbv = cw_ref[:, pl.ds(vcol, FFN_CHUNK)], cb_ref[:, pl.ds(vcol, FFN_CHUNK)]
        for r in range(tm // FFN_ROWS):
            rows = slice(r * FFN_ROWS, (r + 1) * FFN_ROWS)
            if up_slot is not None:
                ug_ref[up_slot, rows, :] = _dot(h_ref[rows, :], wg_bf[...])
                uv_ref[up_slot, rows, :] = _dot(h_ref[rows, :], wv_bf[...])
            if down_f is not None:
                ext_g = jnp.concatenate([rows_before(ug_ref, down_slot, pg_ref, down_f, r),
                                         ug_ref[down_slot, rows, :]], axis=0)
                ext_v = jnp.concatenate([rows_before(uv_ref, down_slot, pv_ref, down_f, r),
                                         uv_ref[down_slot, rows, :]], axis=0)
                gate = _causal_conv(ext_g, cwg, cbg)
                val = _causal_conv(ext_v, cwv, cbv)
                act = (gate * jax.nn.sigmoid(gate) * val).astype(BF16)
                o_ref[rows, :] += _dot(act, wd_bf[down_slot])
        if down_f is not None:
            pg_ref[down_f] = ug_ref[down_slot, tm - SUBLANES:, :]
            pv_ref[down_f] = uv_ref[down_slot, tm - SUBLANES:, :]

    @pl.when(m == 0)
    def _():
        for c in weight_copies(0, 0):
            c.start()

    x_copy = pltpu.make_async_copy(x_hbm.at[pl.ds(pl.multiple_of(m * tm, tm), tm), :], o_ref, sem_x)
    x_copy.start()
    x_copy.wait()
    h_ref[...] = _rms(o_ref[...], g_ref[...]).astype(BF16)

    @pl.when(m % tiles_per_seq == 0)
    def _():
        pg_ref[...] = jnp.zeros_like(pg_ref)
        pv_ref[...] = jnp.zeros_like(pv_ref)

    stage(0, 0, None, None)

    def pair(i, carry):
        f = 2 * i + 1
        stage(f, 1, f - 1, 0)
        stage(f + 1, 0, f, 1)
        return carry

    lax.fori_loop(0, (nf - 2) // 2, pair, 0)
    stage(nf - 1, 1, nf - 2, 0)
    stage(None, None, nf - 1, 1)


def _ffn(x2, g, w_up, cw, cb, w_down, layer, seq):
    t, d = x2.shape
    d_ff = w_down.shape[1]
    nf = d_ff // FFN_CHUNK
    assert nf % 2 == 0 and seq % TM_FFN == 0
    return pl.pallas_call(
        functools.partial(_ffn_kernel, layer=layer, tiles_per_seq=seq // TM_FFN),
        grid=(t // TM_FFN,),
        in_specs=[
            pl.BlockSpec(memory_space=pl.ANY),
            _resident((1, d)),
            pl.BlockSpec(memory_space=pl.ANY),
            _resident(cw.shape),
            _resident(cb.shape),
            pl.BlockSpec(memory_space=pl.ANY),
        ],
        out_specs=pl.BlockSpec((TM_FFN, d), lambda m: (m, 0)),
        out_shape=jax.ShapeDtypeStruct((t, d), F32),
        scratch_shapes=[
            pltpu.VMEM((TM_FFN, d), BF16),
            pltpu.VMEM((2, d, FFN_CHUNK), F32),
            pltpu.VMEM((2, d, FFN_CHUNK), F32),
            pltpu.VMEM((2, FFN_CHUNK, d), F32),
            pltpu.VMEM((d, FFN_CHUNK), BF16),
            pltpu.VMEM((d, FFN_CHUNK), BF16),
            pltpu.VMEM((2, FFN_CHUNK, d), BF16),
            pltpu.VMEM((2, TM_FFN, FFN_CHUNK), F32),
            pltpu.VMEM((2, TM_FFN, FFN_CHUNK), F32),
            pltpu.VMEM((nf, SUBLANES, FFN_CHUNK), F32),
            pltpu.VMEM((nf, SUBLANES, FFN_CHUNK), F32),
            pltpu.SemaphoreType.DMA(()),
            pltpu.SemaphoreType.DMA((2, 3)),
        ],
        compiler_params=_params(("arbitrary",)),
        name="conv_ffn",
    )(x2, g, w_up, cw, cb, w_down)


def kernel(x, attn_norm_g, w_in, q_norm_g, k_norm_g, sgu_norm_g, sgu_w, sgu_b, out_norm_a_g,
           out_norm_b_g, w_out, ffn_norm_g, w_up, conv_w, conv_b, w_down):
    batch, seq, d = x.shape
    depth = w_in.shape[0]
    assert seq % TM_FFN == 0 and seq % TQ == 0 and seq % TM_IN == 0 and TM_IN % CHUNK == 0
    assert TQ == TK
    x2 = x.reshape(batch * seq, d)
    for l in range(depth):
        q, k, v, sg = _in_proj(
            x2, attn_norm_g[l][None], w_in[l].astype(BF16), q_norm_g[l][None], k_norm_g[l][None],
            sgu_norm_g[l], sgu_w[l], sgu_b[l][:, :, None], out_norm_b_g[l])
        att = _attention(
            q, k, v, out_norm_a_g[l].reshape(N_HEADS // HEADS_PER_STEP, HEADS_PER_STEP, HEAD_DIM),
            batch, seq)
        x2 = _out_proj(att, sg, x2, w_out[l].astype(BF16))
        x2 = _ffn(x2, ffn_norm_g[l][None], w_up, conv_w[l], conv_b[l][None], w_down, l, seq)
    return x2.reshape(batch, seq, d)
```

```python
import functools

import jax
import jax.numpy as jnp
from jax import lax
from jax.experimental import pallas as pl
from jax.experimental.pallas import tpu as pltpu

HEAD_DIM = 128
N_HEADS = 8
N_GROUPS = 8
CHUNK = 128
CONV_WIDTH = 3
EPS = 1e-6
F32_EXP_UNDERFLOW = -105.0

V7X_VMEM_BYTES = 64 * 1024 * 1024
VMEM_LIMIT_BYTES = V7X_VMEM_BYTES - 8 * 1024 * 1024
SUBLANES = 8

W_LOAD_COLS = 256
TM_IN = 512
TQ = 256
TK = 256
HEADS_PER_STEP = 4
TM_OUT = 512
TM_FFN = 2048
FFN_CHUNK = 256
FFN_ROWS = 256

BF16 = jnp.bfloat16
F32 = jnp.float32


def _rms(x, g):
    ms = jnp.mean(x * x, axis=-1, keepdims=True)
    return x * lax.rsqrt(ms + EPS) * g


def _gelu(x):
    return 0.5 * x * (1.0 + lax.erf(x * 0.7071067811865476))


def _dot(a, b):
    return jnp.dot(a, b, preferred_element_type=F32)


def _resident(shape):
    return pl.BlockSpec(shape, lambda *_: (0,) * len(shape), pipeline_mode=pl.Buffered(1))


def _params(semantics):
    return pltpu.CompilerParams(dimension_semantics=semantics, vmem_limit_bytes=VMEM_LIMIT_BYTES)


def _load_weights(w_hbm, layer, w_ref, stage_ref, sem):
    n_chunks = w_ref.shape[1] // W_LOAD_COLS

    def copy(c):
        return pltpu.make_async_copy(w_hbm.at[layer, :, pl.ds(c * W_LOAD_COLS, W_LOAD_COLS)],
                                     stage_ref.at[c % 2], sem.at[c % 2])

    copy(0).start()
    for c in range(n_chunks):
        copy(c).wait()
        if c + 1 < n_chunks:
            copy(c + 1).start()
        w_ref[:, c * W_LOAD_COLS:(c + 1) * W_LOAD_COLS] = stage_ref[c % 2].astype(BF16)


def _weight_scratch(w):
    k, n = w.shape[1:]
    return [pltpu.VMEM((k, n), BF16), pltpu.VMEM((2, k, W_LOAD_COLS), F32),
            pltpu.SemaphoreType.DMA((2,))]


def _in_proj_kernel(x_ref, g_ref, w_hbm, qg_ref, kg_ref, sng_ref, sw_ref, sb_ref, ong_ref,
                    q_ref, k_ref, v_ref, sg_ref, w_ref, stage_ref, sem, *, layer, sb_width):
    @pl.when(pl.program_id(0) == 0)
    def _():
        _load_weights(w_hbm, layer, w_ref, stage_ref, sem)

    tm = x_ref.shape[0]
    h = _rms(x_ref[...], g_ref[...]).astype(BF16)

    def cols(i):
        return _dot(h, w_ref[:, i * sb_width:(i + 1) * sb_width])

    pq = cols(0)
    pk = cols(1)
    for hh in range(N_HEADS):
        sl = slice(hh * HEAD_DIM, (hh + 1) * HEAD_DIM)
        q_ref[:, sl] = _rms(pq[:, sl], qg_ref[...]).astype(BF16)
        k_ref[:, sl] = _rms(pk[:, sl], kg_ref[...]).astype(BF16)
    v_ref[...] = cols(2).astype(BF16)

    u = _gelu(cols(3))
    vg = _gelu(cols(4))
    row = lax.broadcasted_iota(jnp.int32, (CHUNK, CHUNK), 0)
    col = lax.broadcasted_iota(jnp.int32, (CHUNK, CHUNK), 1)
    causal = col <= row
    for g in range(N_GROUPS):
        sl = slice(g * HEAD_DIM, (g + 1) * HEAD_DIM)
        vs = _rms(vg[:, sl], sng_ref[g:g + 1, :]).astype(BF16)
        wg = jnp.where(causal, sw_ref[g], 0.0).astype(BF16)
        bias = sb_ref[g]
        for c in range(tm // CHUNK):
            rs = slice(c * CHUNK, (c + 1) * CHUNK)
            mixed = _dot(wg, vs[rs, :]) + bias
            sg = u[rs, sl] * mixed
            sg_ref[rs, sl] = _rms(sg, ong_ref[g:g + 1, :]).astype(BF16)


def _in_proj(x2, g, w_in, layer, qg, kg, sng, sw, sb, ong):
    t, d = x2.shape
    sbw = N_HEADS * HEAD_DIM
    row_blk = lambda w: pl.BlockSpec((TM_IN, w), lambda m: (m, 0))
    out = jax.ShapeDtypeStruct((t, sbw), BF16)
    return pl.pallas_call(
        functools.partial(_in_proj_kernel, layer=layer, sb_width=sbw),
        grid=(t // TM_IN,),
        in_specs=[
            row_blk(d),
            _resident((1, d)),
            pl.BlockSpec(memory_space=pl.ANY),
            _resident((1, HEAD_DIM)),
            _resident((1, HEAD_DIM)),
            _resident((N_GROUPS, HEAD_DIM)),
            _resident((N_GROUPS, CHUNK, CHUNK)),
            _resident((N_GROUPS, CHUNK, 1)),
            _resident((N_GROUPS, HEAD_DIM)),
        ],
        out_specs=[row_blk(sbw)] * 4,
        out_shape=[out] * 4,
        scratch_shapes=_weight_scratch(w_in),
        compiler_params=_params(("arbitrary",)),
        name="in_proj",
    )(x2, g, w_in, qg, kg, sng, sw, sb, ong)


def _attn_kernel(q_ref, k_ref, v_ref, g_ref, o_ref, acc_ref, tail_ref):
    qi = pl.program_id(2)
    scale = HEAD_DIM ** -0.5
    row = lax.broadcasted_iota(jnp.int32, (TK, TK), 0)
    col = lax.broadcasted_iota(jnp.int32, (TK, TK), 1)
    later = jnp.where(row > col, 1.0, 0.0).astype(BF16)
    later2 = jnp.concatenate([later, later], axis=0)
    causal = col < row

    acc_ref[...] = jnp.zeros_like(acc_ref)
    tail_ref[...] = jnp.zeros_like(tail_ref)

    heads = range(HEADS_PER_STEP)
    lanes = [slice(hh * HEAD_DIM, (hh + 1) * HEAD_DIM) for hh in heads]

    def block(j, masked):
        start = pl.multiple_of(j * TK, TK)
        z = [lax.dot_general(q_ref[:, cs], k_ref[pl.ds(start, TK), cs], (((1,), (1,)), ((), ())),
                             preferred_element_type=F32) * scale for cs in lanes]
        log_beta, log_1m, split = [], [], []
        for hh in heads:
            lb = jnp.minimum(z[hh], 0.0) - jnp.log(1.0 + jnp.exp(-jnp.abs(z[hh])))
            l1 = lb - z[hh]
            if masked:
                l1 = jnp.where(causal, l1, 0.0)
            hi = l1.astype(BF16)
            split.append(jnp.concatenate([hi, (l1 - hi.astype(F32)).astype(BF16)], axis=1))
            log_beta.append(lb)
            log_1m.append(l1)
        inner = [_dot(split[hh], later2) for hh in heads]
        a = []
        live = None
        for hh in heads:
            carried = tail_ref[hh]
            tail = inner[hh] + jnp.concatenate([carried] * (TK // HEAD_DIM), axis=1)
            w = jnp.exp(log_beta[hh] + tail)
            if masked:
                w = jnp.where(causal, w, 0.0)
            a.append(w.astype(BF16))
            swept = carried + jnp.sum(log_1m[hh], axis=1, keepdims=True)
            tail_ref[hh] = swept
            live = swept if live is None else jnp.maximum(live, swept)
        for hh in heads:
            acc_ref[hh] += _dot(a[hh], v_ref[pl.ds(start, TK), lanes[hh]])
        return jnp.max(live) > F32_EXP_UNDERFLOW

    def cond(state):
        i, live = state
        return jnp.logical_and(i < qi, live)

    def body(state):
        i, _ = state
        return i + 1, block(qi - 1 - i, False)

    lax.while_loop(cond, body, (jnp.int32(0), block(qi, True)))
    for hh in range(HEADS_PER_STEP):
        cs = slice(hh * HEAD_DIM, (hh + 1) * HEAD_DIM)
        o_ref[:, cs] = _rms(acc_ref[hh], g_ref[0, hh:hh + 1, :]).astype(BF16)


def _attention(q, k, v, ong, batch, seq):
    t, w = q.shape
    nq = seq // TQ
    hw = HEADS_PER_STEP * HEAD_DIM
    return pl.pallas_call(
        _attn_kernel,
        grid=(batch, N_HEADS // HEADS_PER_STEP, nq),
        in_specs=[
            pl.BlockSpec((TQ, hw), lambda b, h, i: (b * nq + i, h)),
            pl.BlockSpec((seq, hw), lambda b, h, i: (b, h)),
            pl.BlockSpec((seq, hw), lambda b, h, i: (b, h)),
            pl.BlockSpec((1, HEADS_PER_STEP, HEAD_DIM), lambda b, h, i: (h, 0, 0)),
        ],
        out_specs=pl.BlockSpec((TQ, hw), lambda b, h, i: (b * nq + i, h)),
        out_shape=jax.ShapeDtypeStruct((t, w), BF16),
        scratch_shapes=[pltpu.VMEM((HEADS_PER_STEP, TQ, HEAD_DIM), F32),
                        pltpu.VMEM((HEADS_PER_STEP, TQ, HEAD_DIM), F32)],
        compiler_params=_params(("arbitrary", "arbitrary", "arbitrary")),
        name="sb_attn",
    )(q, k, v, ong)


def _out_proj_kernel(att_ref, sg_ref, x_ref, w_hbm, o_ref, w_ref, stage_ref, sem, *, layer):
    @pl.when(pl.program_id(0) == 0)
    def _():
        _load_weights(w_hbm, layer, w_ref, stage_ref, sem)

    half = att_ref.shape[1]
    o_ref[...] = x_ref[...] + (_dot(att_ref[...], w_ref[:half, :]) + _dot(sg_ref[...], w_ref[half:, :]))


def _out_proj(att, sg, x2, w_out, layer):
    t, d = x2.shape
    half = att.shape[1]
    return pl.pallas_call(
        functools.partial(_out_proj_kernel, layer=layer),
        grid=(t // TM_OUT,),
        in_specs=[
            pl.BlockSpec((TM_OUT, half), lambda m: (m, 0)),
            pl.BlockSpec((TM_OUT, half), lambda m: (m, 0)),
            pl.BlockSpec((TM_OUT, d), lambda m: (m, 0)),
            pl.BlockSpec(memory_space=pl.ANY),
        ],
        out_specs=pl.BlockSpec((TM_OUT, d), lambda m: (m, 0)),
        out_shape=jax.ShapeDtypeStruct((t, d), F32),
        scratch_shapes=_weight_scratch(w_out),
        compiler_params=_params(("arbitrary",)),
        name="out_proj",
    )(att, sg, x2, w_out)


def _causal_conv(ext, cw, cb):
    u = ext[SUBLANES:, :]
    u1 = pltpu.roll(ext, 1, axis=0)[SUBLANES:, :]
    u2 = pltpu.roll(ext, 2, axis=0)[SUBLANES:, :]
    return cb + u2 * cw[0:1, :] + u1 * cw[1:2, :] + u * cw[2:3, :]


def _ffn_kernel(x_hbm, g_ref, wup_hbm, cw_ref, cb_ref, wdn_hbm, o_hbm,
                acc_ref, h_ref, wg_st, wv_st, wd_st, wg_bf, wv_bf, wd_bf, ug_ref, uv_ref,
                pg_ref, pv_ref, sem_x, sem_o, sem_w, *, layer, tiles_per_seq):
    m = pl.program_id(0)
    n_steps = pl.num_programs(0)
    tm = acc_ref.shape[0]
    n_blocks = tm // FFN_ROWS
    d_ff = wdn_hbm.shape[1]
    nf = d_ff // FFN_CHUNK
    base = pl.multiple_of(m * tm, tm)

    def weight_copies(f):
        col = pl.multiple_of(f * FFN_CHUNK, FFN_CHUNK)
        vcol = pl.multiple_of(d_ff + col, FFN_CHUNK)
        return (
            pltpu.make_async_copy(wup_hbm.at[layer, :, pl.ds(col, FFN_CHUNK)], wg_st, sem_w.at[0]),
            pltpu.make_async_copy(wup_hbm.at[layer, :, pl.ds(vcol, FFN_CHUNK)], wv_st, sem_w.at[1]),
            pltpu.make_async_copy(wdn_hbm.at[layer, pl.ds(col, FFN_CHUNK), :], wd_st, sem_w.at[2]),
        )

    def take_weights(f, slot):
        for c in weight_copies(f):
            c.wait()
        wg_bf[...] = wg_st[...].astype(BF16)
        wv_bf[...] = wv_st[...].astype(BF16)
        wd_bf[slot] = wd_st[...].astype(BF16)
        nxt = f + 1

        @pl.when(jnp.logical_or(nxt < nf, m < n_steps - 1))
        def _():
            for c in weight_copies(jnp.where(nxt < nf, nxt, 0)):
                c.start()

    def out_copy(r):
        rows = pl.ds(r * FFN_ROWS, FFN_ROWS)
        return pltpu.make_async_copy(acc_ref.at[rows, :],
                                     o_hbm.at[pl.ds(base + r * FFN_ROWS, FFN_ROWS), :], sem_o.at[r])

    def rows_before(u_ref, slot, prev_ref, f, r):
        if r == 0:
            return prev_ref[f]
        return u_ref[slot, r * FFN_ROWS - SUBLANES:r * FFN_ROWS, :]

    def stage(up_f, up_slot, down_f, down_slot, last=False):
        if up_slot is not None:
            take_weights(up_f, up_slot)
        if down_f is not None:
            col = pl.multiple_of(down_f * FFN_CHUNK, FFN_CHUNK)
            vcol = pl.multiple_of(d_ff + col, FFN_CHUNK)
            cwg, cbg = cw_ref[:, pl.ds(col, FFN_CHUNK)], cb_ref[:, pl.ds(col, FFN_CHUNK)]
            cwv, cbv = cw_ref[:, pl.ds(vcol, FFN_CHUNK)], cb_ref[:, pl.ds(vcol, FFN_CHUNK)]
        for r in range(n_blocks):
            rows = slice(r * FFN_ROWS, (r + 1) * FFN_ROWS)
            if up_slot is not None:
                ug_ref[up_slot, rows, :] = _dot(h_ref[rows, :], wg_bf[...])
                uv_ref[up_slot, rows, :] = _dot(h_ref[rows, :], wv_bf[...])
            if down_f is not None:
                ext_g = jnp.concatenate([rows_before(ug_ref, down_slot, pg_ref, down_f, r),
                                         ug_ref[down_slot, rows, :]], axis=0)
                ext_v = jnp.concatenate([rows_before(uv_ref, down_slot, pv_ref, down_f, r),
                                         uv_ref[down_slot, rows, :]], axis=0)
                gate = _causal_conv(ext_g, cwg, cbg)
                val = _causal_conv(ext_v, cwv, cbv)
                act = (gate * jax.nn.sigmoid(gate) * val).astype(BF16)
                acc_ref[rows, :] += _dot(act, wd_bf[down_slot])
                if last:
                    out_copy(r).start()
        if down_f is not None:
            pg_ref[down_f] = ug_ref[down_slot, tm - SUBLANES:, :]
            pv_ref[down_f] = uv_ref[down_slot, tm - SUBLANES:, :]

    @pl.when(m == 0)
    def _():
        for c in weight_copies(0):
            c.start()

    x_copies = [pltpu.make_async_copy(x_hbm.at[pl.ds(base + r * FFN_ROWS, FFN_ROWS), :],
                                      acc_ref.at[pl.ds(r * FFN_ROWS, FFN_ROWS), :], sem_x.at[r])
                for r in range(n_blocks)]
    for c in x_copies:
        c.start()
    for r, c in enumerate(x_copies):
        rows = slice(r * FFN_ROWS, (r + 1) * FFN_ROWS)
        c.wait()
        h_ref[rows, :] = _rms(acc_ref[rows, :], g_ref[...]).astype(BF16)

    @pl.when(m % tiles_per_seq == 0)
    def _():
        pg_ref[...] = jnp.zeros_like(pg_ref)
        pv_ref[...] = jnp.zeros_like(pv_ref)

    stage(0, 0, None, None)

    def pair(i, carry):
        f = 2 * i + 1
        stage(f, 1, f - 1, 0)
        stage(f + 1, 0, f, 1)
        return carry

    lax.fori_loop(0, (nf - 2) // 2, pair, 0)
    stage(nf - 1, 1, nf - 2, 0)
    stage(None, None, nf - 1, 1, last=True)
    for r in range(n_blocks):
        out_copy(r).wait()


def _ffn(x2, g, w_up, cw, cb, w_down, layer, seq):
    t, d = x2.shape
    d_ff = w_down.shape[1]
    nf = d_ff // FFN_CHUNK
    assert nf % 2 == 0 and seq % TM_FFN == 0 and TM_FFN % FFN_ROWS == 0
    return pl.pallas_call(
        functools.partial(_ffn_kernel, layer=layer, tiles_per_seq=seq // TM_FFN),
        grid=(t // TM_FFN,),
        in_specs=[
            pl.BlockSpec(memory_space=pl.ANY),
            _resident((1, d)),
            pl.BlockSpec(memory_space=pl.ANY),
            _resident(cw.shape),
            _resident(cb.shape),
            pl.BlockSpec(memory_space=pl.ANY),
        ],
        out_specs=pl.BlockSpec(memory_space=pl.ANY),
        out_shape=jax.ShapeDtypeStruct((t, d), F32),
        scratch_shapes=[
            pltpu.VMEM((TM_FFN, d), F32),
            pltpu.VMEM((TM_FFN, d), BF16),
            pltpu.VMEM((d, FFN_CHUNK), F32),
            pltpu.VMEM((d, FFN_CHUNK), F32),
            pltpu.VMEM((FFN_CHUNK, d), F32),
            pltpu.VMEM((d, FFN_CHUNK), BF16),
            pltpu.VMEM((d, FFN_CHUNK), BF16),
            pltpu.VMEM((2, FFN_CHUNK, d), BF16),
            pltpu.VMEM((2, TM_FFN, FFN_CHUNK), F32),
            pltpu.VMEM((2, TM_FFN, FFN_CHUNK), F32),
            pltpu.VMEM((nf, SUBLANES, FFN_CHUNK), F32),
            pltpu.VMEM((nf, SUBLANES, FFN_CHUNK), F32),
            pltpu.SemaphoreType.DMA((TM_FFN // FFN_ROWS,)),
            pltpu.SemaphoreType.DMA((TM_FFN // FFN_ROWS,)),
            pltpu.SemaphoreType.DMA((3,)),
        ],
        compiler_params=_params(("arbitrary",)),
        name="conv_ffn",
    )(x2, g, w_up, cw, cb, w_down)


def kernel(x, attn_norm_g, w_in, q_norm_g, k_norm_g, sgu_norm_g, sgu_w, sgu_b, out_norm_a_g,
           out_norm_b_g, w_out, ffn_norm_g, w_up, conv_w, conv_b, w_down):
    batch, seq, d = x.shape
    depth = w_in.shape[0]
    assert seq % TM_FFN == 0 and seq % TQ == 0 and seq % TM_IN == 0 and TM_IN % CHUNK == 0
    assert TQ == TK
    x2 = x.reshape(batch * seq, d)
    for l in range(depth):
        q, k, v, sg = _in_proj(
            x2, attn_norm_g[l][None], w_in, l, q_norm_g[l][None], k_norm_g[l][None],
            sgu_norm_g[l], sgu_w[l], sgu_b[l][:, :, None], out_norm_b_g[l])
        att = _attention(
            q, k, v, out_norm_a_g[l].reshape(N_HEADS // HEADS_PER_STEP, HEADS_PER_STEP, HEAD_DIM),
            batch, seq)
        x2 = _out_proj(att, sg, x2, w_out, l)
        x2 = _ffn(x2, ffn_norm_g[l][None], w_up, conv_w[l], conv_b[l][None], w_down, l, seq)
    return x2.reshape(batch, seq, d)
```

```python
import functools

import jax
import jax.numpy as jnp
from jax import lax
from jax.experimental import pallas as pl
from jax.experimental.pallas import tpu as pltpu

HEAD_DIM = 128
N_HEADS = 8
N_GROUPS = 8
CHUNK = 128
CONV_WIDTH = 3
EPS = 1e-6
F32_EXP_UNDERFLOW = -105.0

V7X_VMEM_BYTES = 64 * 1024 * 1024
VMEM_LIMIT_BYTES = V7X_VMEM_BYTES - 8 * 1024 * 1024
SUBLANES = 8

W_LOAD_COLS = 256
TM_IN = 512
TQ = 256
TK = 256
HEADS_PER_STEP = 4
TM_OUT = 512
W_DMA_PARTS = 2
TM_FFN = 1024
FFN_CHUNK = 256
FFN_ROWS = 256

BF16 = jnp.bfloat16
F32 = jnp.float32


def _rms(x, g):
    ms = jnp.mean(x * x, axis=-1, keepdims=True)
    return x * lax.rsqrt(ms + EPS) * g


def _gelu(x):
    return 0.5 * x * (1.0 + lax.erf(x * 0.7071067811865476))


def _dot(a, b):
    return jnp.dot(a, b, preferred_element_type=F32)


def _resident(shape):
    return pl.BlockSpec(shape, lambda *_: (0,) * len(shape), pipeline_mode=pl.Buffered(1))


def _params(semantics):
    return pltpu.CompilerParams(dimension_semantics=semantics, vmem_limit_bytes=VMEM_LIMIT_BYTES)


def _load_weights(w_hbm, layer, w_ref, stage_ref, sem):
    n_chunks = w_ref.shape[1] // W_LOAD_COLS
    part = w_ref.shape[0] // W_DMA_PARTS

    def copies(c):
        cols = pl.ds(c * W_LOAD_COLS, W_LOAD_COLS)
        return [pltpu.make_async_copy(w_hbm.at[layer, pl.ds(p * part, part), cols],
                                      stage_ref.at[c % 2, pl.ds(p * part, part), :],
                                      sem.at[c % 2, p]) for p in range(W_DMA_PARTS)]

    def start(c):
        for p, cp in enumerate(copies(c)):
            cp.start(priority=p % 2)

    start(0)
    for c in range(n_chunks):
        if c + 1 < n_chunks:
            start(c + 1)
        for cp in copies(c):
            cp.wait()
        w_ref[:, c * W_LOAD_COLS:(c + 1) * W_LOAD_COLS] = stage_ref[c % 2].astype(BF16)


def _weight_scratch(w):
    k, n = w.shape[1:]
    return [pltpu.VMEM((k, n), BF16), pltpu.VMEM((2, k, W_LOAD_COLS), F32),
            pltpu.SemaphoreType.DMA((2, W_DMA_PARTS))]


def _in_proj_kernel(x_ref, g_ref, w_hbm, qg_ref, kg_ref, sng_ref, sw_ref, sb_ref, ong_ref,
                    q_ref, k_ref, v_ref, sg_ref, w_ref, stage_ref, sem, *, layer, sb_width):
    @pl.when(pl.program_id(0) == 0)
    def _():
        _load_weights(w_hbm, layer, w_ref, stage_ref, sem)

    tm = x_ref.shape[0]
    h = _rms(x_ref[...], g_ref[...]).astype(BF16)

    def cols(i):
        return _dot(h, w_ref[:, i * sb_width:(i + 1) * sb_width])

    pq = cols(0)
    pk = cols(1)
    for hh in range(N_HEADS):
        sl = slice(hh * HEAD_DIM, (hh + 1) * HEAD_DIM)
        q_ref[:, sl] = _rms(pq[:, sl], qg_ref[...]).astype(BF16)
        k_ref[:, sl] = _rms(pk[:, sl], kg_ref[...]).astype(BF16)
    v_ref[...] = cols(2).astype(BF16)

    u = _gelu(cols(3))
    vg = _gelu(cols(4))
    row = lax.broadcasted_iota(jnp.int32, (CHUNK, CHUNK), 0)
    col = lax.broadcasted_iota(jnp.int32, (CHUNK, CHUNK), 1)
    causal = col <= row
    for g in range(N_GROUPS):
        sl = slice(g * HEAD_DIM, (g + 1) * HEAD_DIM)
        vs = _rms(vg[:, sl], sng_ref[g:g + 1, :]).astype(BF16)
        wg = jnp.where(causal, sw_ref[g], 0.0).astype(BF16)
        bias = sb_ref[g]
        for c in range(tm // CHUNK):
            rs = slice(c * CHUNK, (c + 1) * CHUNK)
            mixed = _dot(wg, vs[rs, :]) + bias
            sg = u[rs, sl] * mixed
            sg_ref[rs, sl] = _rms(sg, ong_ref[g:g + 1, :]).astype(BF16)


def _in_proj(x2, g, w_in, layer, qg, kg, sng, sw, sb, ong):
    t, d = x2.shape
    sbw = N_HEADS * HEAD_DIM
    row_blk = lambda w: pl.BlockSpec((TM_IN, w), lambda m: (m, 0))
    out = jax.ShapeDtypeStruct((t, sbw), BF16)
    return pl.pallas_call(
        functools.partial(_in_proj_kernel, layer=layer, sb_width=sbw),
        grid=(t // TM_IN,),
        in_specs=[
            row_blk(d),
            _resident((1, d)),
            pl.BlockSpec(memory_space=pl.ANY),
            _resident((1, HEAD_DIM)),
            _resident((1, HEAD_DIM)),
            _resident((N_GROUPS, HEAD_DIM)),
            _resident((N_GROUPS, CHUNK, CHUNK)),
            _resident((N_GROUPS, CHUNK, 1)),
            _resident((N_GROUPS, HEAD_DIM)),
        ],
        out_specs=[row_blk(sbw)] * 4,
        out_shape=[out] * 4,
        scratch_shapes=_weight_scratch(w_in),
        compiler_params=_params(("arbitrary",)),
        name="in_proj",
    )(x2, g, w_in, qg, kg, sng, sw, sb, ong)


def _attn_kernel(q_ref, k_ref, v_ref, g_ref, o_ref, acc_ref, tail_ref):
    qi = pl.program_id(2)
    scale = HEAD_DIM ** -0.5
    row = lax.broadcasted_iota(jnp.int32, (TK, TK), 0)
    col = lax.broadcasted_iota(jnp.int32, (TK, TK), 1)
    later = jnp.where(row > col, 1.0, 0.0).astype(BF16)
    later2 = jnp.concatenate([later, later], axis=0)
    causal = col < row

    acc_ref[...] = jnp.zeros_like(acc_ref)
    tail_ref[...] = jnp.zeros_like(tail_ref)

    heads = range(HEADS_PER_STEP)
    lanes = [slice(hh * HEAD_DIM, (hh + 1) * HEAD_DIM) for hh in heads]

    def block(j, masked):
        start = pl.multiple_of(j * TK, TK)
        z = [lax.dot_general(q_ref[:, cs], k_ref[pl.ds(start, TK), cs], (((1,), (1,)), ((), ())),
                             preferred_element_type=F32) * scale for cs in lanes]
        log_beta, log_1m, split = [], [], []
        for hh in heads:
            lb = jnp.minimum(z[hh], 0.0) - jnp.log(1.0 + jnp.exp(-jnp.abs(z[hh])))
            l1 = lb - z[hh]
            if masked:
                l1 = jnp.where(causal, l1, 0.0)
            hi = l1.astype(BF16)
            split.append(jnp.concatenate([hi, (l1 - hi.astype(F32)).astype(BF16)], axis=1))
            log_beta.append(lb)
            log_1m.append(l1)
        inner = [_dot(split[hh], later2) for hh in heads]
        a = []
        live = None
        for hh in heads:
            carried = tail_ref[hh]
            tail = inner[hh] + jnp.concatenate([carried] * (TK // HEAD_DIM), axis=1)
            w = jnp.exp(log_beta[hh] + tail)
            if masked:
                w = jnp.where(causal, w, 0.0)
            a.append(w.astype(BF16))
            swept = carried + jnp.sum(log_1m[hh], axis=1, keepdims=True)
            tail_ref[hh] = swept
            live = swept if live is None else jnp.maximum(live, swept)
        for hh in heads:
            acc_ref[hh] += _dot(a[hh], v_ref[pl.ds(start, TK), lanes[hh]])
        return jnp.max(live) > F32_EXP_UNDERFLOW

    def cond(state):
        i, live = state
        return jnp.logical_and(i < qi, live)

    def body(state):
        i, _ = state
        return i + 1, block(qi - 1 - i, False)

    lax.while_loop(cond, body, (jnp.int32(0), block(qi, True)))
    for hh in range(HEADS_PER_STEP):
        cs = slice(hh * HEAD_DIM, (hh + 1) * HEAD_DIM)
        o_ref[:, cs] = _rms(acc_ref[hh], g_ref[0, hh:hh + 1, :]).astype(BF16)


def _attention(q, k, v, ong, batch, seq):
    t, w = q.shape
    nq = seq // TQ
    hw = HEADS_PER_STEP * HEAD_DIM
    return pl.pallas_call(
        _attn_kernel,
        grid=(batch, N_HEADS // HEADS_PER_STEP, nq),
        in_specs=[
            pl.BlockSpec((TQ, hw), lambda b, h, i: (b * nq + i, h)),
            pl.BlockSpec((seq, hw), lambda b, h, i: (b, h)),
            pl.BlockSpec((seq, hw), lambda b, h, i: (b, h)),
            pl.BlockSpec((1, HEADS_PER_STEP, HEAD_DIM), lambda b, h, i: (h, 0, 0)),
        ],
        out_specs=pl.BlockSpec((TQ, hw), lambda b, h, i: (b * nq + i, h)),
        out_shape=jax.ShapeDtypeStruct((t, w), BF16),
        scratch_shapes=[pltpu.VMEM((HEADS_PER_STEP, TQ, HEAD_DIM), F32),
                        pltpu.VMEM((HEADS_PER_STEP, TQ, HEAD_DIM), F32)],
        compiler_params=_params(("arbitrary", "arbitrary", "arbitrary")),
        name="sb_attn",
    )(q, k, v, ong)


def _out_proj_kernel(att_ref, sg_ref, x_ref, w_hbm, o_ref, w_ref, stage_ref, sem, *, layer):
    @pl.when(pl.program_id(0) == 0)
    def _():
        _load_weights(w_hbm, layer, w_ref, stage_ref, sem)

    half = att_ref.shape[1]
    o_ref[...] = x_ref[...] + (_dot(att_ref[...], w_ref[:half, :]) + _dot(sg_ref[...], w_ref[half:, :]))


def _out_proj(att, sg, x2, w_out, layer):
    t, d = x2.shape
    half = att.shape[1]
    return pl.pallas_call(
        functools.partial(_out_proj_kernel, layer=layer),
        grid=(t // TM_OUT,),
        in_specs=[
            pl.BlockSpec((TM_OUT, half), lambda m: (m, 0)),
            pl.BlockSpec((TM_OUT, half), lambda m: (m, 0)),
            pl.BlockSpec((TM_OUT, d), lambda m: (m, 0)),
            pl.BlockSpec(memory_space=pl.ANY),
        ],
        out_specs=pl.BlockSpec((TM_OUT, d), lambda m: (m, 0)),
        out_shape=jax.ShapeDtypeStruct((t, d), F32),
        scratch_shapes=_weight_scratch(w_out),
        compiler_params=_params(("arbitrary",)),
        name="out_proj",
    )(att, sg, x2, w_out)


def _causal_conv(ext, cw, cb):
    u = ext[SUBLANES:, :]
    u1 = pltpu.roll(ext, 1, axis=0)[SUBLANES:, :]
    u2 = pltpu.roll(ext, 2, axis=0)[SUBLANES:, :]
    return cb + u2 * cw[0:1, :] + u1 * cw[1:2, :] + u * cw[2:3, :]


def _ffn_kernel(x_hbm, g_ref, wup_hbm, cw_ref, cb_ref, wdn_hbm, o_hbm,
                acc_ref, h_ref, wg_st, wv_st, wd_st, wg_bf, wv_bf, wd_bf, ug_ref, uv_ref,
                pg_ref, pv_ref, sem_x, sem_o, sem_w, *, layer, tiles_per_seq):
    m = pl.program_id(0)
    n_steps = pl.num_programs(0)
    tm = acc_ref.shape[0]
    n_blocks = tm // FFN_ROWS
    d_ff = wdn_hbm.shape[1]
    nf = d_ff // FFN_CHUNK
    base = pl.multiple_of(m * tm, tm)

    def weight_copies(f, slot):
        col = pl.multiple_of(f * FFN_CHUNK, FFN_CHUNK)
        vcol = pl.multiple_of(d_ff + col, FFN_CHUNK)
        streams = ((wup_hbm.at[layer, :, pl.ds(col, FFN_CHUNK)], wg_st.at[slot]),
                   (wup_hbm.at[layer, :, pl.ds(vcol, FFN_CHUNK)], wv_st.at[slot]),
                   (wdn_hbm.at[layer, pl.ds(col, FFN_CHUNK), :], wd_st.at[slot]))
        copies = []
        for i, (src, dst) in enumerate(streams):
            part = dst.shape[0] // W_DMA_PARTS
            for p in range(W_DMA_PARTS):
                rows = pl.ds(p * part, part)
                copies.append((pltpu.make_async_copy(src.at[rows, :], dst.at[rows, :],
                                                     sem_w.at[slot, i * W_DMA_PARTS + p]), p % 2))
        return copies

    def take_weights(f, slot):
        for c, _ in weight_copies(f, slot):
            c.wait()
        nxt = f + 1

        @pl.when(jnp.logical_or(nxt < nf, m < n_steps - 1))
        def _():
            for c, priority in weight_copies(jnp.where(nxt < nf, nxt, 0), 1 - slot):
                c.start(priority=priority)

        wg_bf[...] = wg_st[slot].astype(BF16)
        wv_bf[...] = wv_st[slot].astype(BF16)
        wd_bf[slot] = wd_st[slot].astype(BF16)

    def out_copy(r):
        rows = pl.ds(r * FFN_ROWS, FFN_ROWS)
        return pltpu.make_async_copy(acc_ref.at[rows, :],
                                     o_hbm.at[pl.ds(base + r * FFN_ROWS, FFN_ROWS), :], sem_o.at[r])

    def rows_before(u_ref, slot, prev_ref, f, r):
        if r == 0:
            return prev_ref[f]
        return u_ref[slot, r * FFN_ROWS - SUBLANES:r * FFN_ROWS, :]

    def stage(up_f, up_slot, down_f, down_slot, last=False):
        if up_slot is not None:
            take_weights(up_f, up_slot)
        if down_f is not None:
            col = pl.multiple_of(down_f * FFN_CHUNK, FFN_CHUNK)
            vcol = pl.multiple_of(d_ff + col, FFN_CHUNK)
            cwg, cbg = cw_ref[:, pl.ds(col, FFN_CHUNK)], cb_ref[:, pl.ds(col, FFN_CHUNK)]
            cwv, cbv = cw_ref[:, pl.ds(vcol, FFN_CHUNK)], cb_ref[:, pl.ds(vcol, FFN_CHUNK)]
        for r in range(n_blocks):
            rows = slice(r * FFN_ROWS, (r + 1) * FFN_ROWS)
            if up_slot is not None:
                ug_ref[up_slot, rows, :] = _dot(h_ref[rows, :], wg_bf[...])
                uv_ref[up_slot, rows, :] = _dot(h_ref[rows, :], wv_bf[...])
            if down_f is not None:
                ext_g = jnp.concatenate([rows_before(ug_ref, down_slot, pg_ref, down_f, r),
                                         ug_ref[down_slot, rows, :]], axis=0)
                ext_v = jnp.concatenate([rows_before(uv_ref, down_slot, pv_ref, down_f, r),
                                         uv_ref[down_slot, rows, :]], axis=0)
                gate = _causal_conv(ext_g, cwg, cbg)
                val = _causal_conv(ext_v, cwv, cbv)
                act = (gate * jax.nn.sigmoid(gate) * val).astype(BF16)
                acc_ref[rows, :] += _dot(act, wd_bf[down_slot])
                if last:
                    out_copy(r).start()
        if down_f is not None:
            pg_ref[down_f] = ug_ref[down_slot, tm - SUBLANES:, :]
            pv_ref[down_f] = uv_ref[down_slot, tm - SUBLANES:, :]

    @pl.when(m == 0)
    def _():
        for c, priority in weight_copies(0, 0):
            c.start(priority=priority)

    x_copies = [pltpu.make_async_copy(x_hbm.at[pl.ds(base + r * FFN_ROWS, FFN_ROWS), :],
                                      acc_ref.at[pl.ds(r * FFN_ROWS, FFN_ROWS), :], sem_x.at[r])
                for r in range(n_blocks)]
    for c in x_copies:
        c.start()
    for r, c in enumerate(x_copies):
        rows = slice(r * FFN_ROWS, (r + 1) * FFN_ROWS)
        c.wait()
        h_ref[rows, :] = _rms(acc_ref[rows, :], g_ref[...]).astype(BF16)

    @pl.when(m % tiles_per_seq == 0)
    def _():
        pg_ref[...] = jnp.zeros_like(pg_ref)
        pv_ref[...] = jnp.zeros_like(pv_ref)

    stage(0, 0, None, None)

    def pair(i, carry):
        f = 2 * i + 1
        stage(f, 1, f - 1, 0)
        stage(f + 1, 0, f, 1)
        return carry

    lax.fori_loop(0, (nf - 2) // 2, pair, 0)
    stage(nf - 1, 1, nf - 2, 0)
    stage(None, None, nf - 1, 1, last=True)
    for r in range(n_blocks):
        out_copy(r).wait()


def _ffn(x2, g, w_up, cw, cb, w_down, layer, seq):
    t, d = x2.shape
    d_ff = w_down.shape[1]
    nf = d_ff // FFN_CHUNK
    assert nf % 2 == 0 and seq % TM_FFN == 0 and TM_FFN % FFN_ROWS == 0
    return pl.pallas_call(
        functools.partial(_ffn_kernel, layer=layer, tiles_per_seq=seq // TM_FFN),
        grid=(t // TM_FFN,),
        in_specs=[
            pl.BlockSpec(memory_space=pl.ANY),
            _resident((1, d)),
            pl.BlockSpec(memory_space=pl.ANY),
            _resident(cw.shape),
            _resident(cb.shape),
            pl.BlockSpec(memory_space=pl.ANY),
        ],
        out_specs=pl.BlockSpec(memory_space=pl.ANY),
        out_shape=jax.ShapeDtypeStruct((t, d), F32),
        scratch_shapes=[
            pltpu.VMEM((TM_FFN, d), F32),
            pltpu.VMEM((TM_FFN, d), BF16),
            pltpu.VMEM((2, d, FFN_CHUNK), F32),
            pltpu.VMEM((2, d, FFN_CHUNK), F32),
            pltpu.VMEM((2, FFN_CHUNK, d), F32),
            pltpu.VMEM((d, FFN_CHUNK), BF16),
            pltpu.VMEM((d, FFN_CHUNK), BF16),
            pltpu.VMEM((2, FFN_CHUNK, d), BF16),
            pltpu.VMEM((2, TM_FFN, FFN_CHUNK), F32),
            pltpu.VMEM((2, TM_FFN, FFN_CHUNK), F32),
            pltpu.VMEM((nf, SUBLANES, FFN_CHUNK), F32),
            pltpu.VMEM((nf, SUBLANES, FFN_CHUNK), F32),
            pltpu.SemaphoreType.DMA((TM_FFN // FFN_ROWS,)),
            pltpu.SemaphoreType.DMA((TM_FFN // FFN_ROWS,)),
            pltpu.SemaphoreType.DMA((2, 3 * W_DMA_PARTS)),
        ],
        compiler_params=_params(("arbitrary",)),
        name="conv_ffn",
    )(x2, g, w_up, cw, cb, w_down)


def kernel(x, attn_norm_g, w_in, q_norm_g, k_norm_g, sgu_norm_g, sgu_w, sgu_b, out_norm_a_g,
           out_norm_b_g, w_out, ffn_norm_g, w_up, conv_w, conv_b, w_down):
    batch, seq, d = x.shape
    depth = w_in.shape[0]
    assert seq % TM_FFN == 0 and seq % TQ == 0 and seq % TM_IN == 0 and TM_IN % CHUNK == 0
    assert TQ == TK
    x2 = x.reshape(batch * seq, d)
    for l in range(depth):
        q, k, v, sg = _in_proj(
            x2, attn_norm_g[l][None], w_in, l, q_norm_g[l][None], k_norm_g[l][None],
            sgu_norm_g[l], sgu_w[l], sgu_b[l][:, :, None], out_norm_b_g[l])
        att = _attention(
            q, k, v, out_norm_a_g[l].reshape(N_HEADS // HEADS_PER_STEP, HEADS_PER_STEP, HEAD_DIM),
            batch, seq)
        x2 = _out_proj(att, sg, x2, w_out, l)
        x2 = _ffn(x2, ffn_norm_g[l][None], w_up, conv_w[l], conv_b[l][None], w_down, l, seq)
    return x2.reshape(batch, seq, d)
```
